```python
import math
import jax, jax.numpy as jnp
from jax import lax
import numpy as np

D_MODEL = 2048
BATCH = 1
SEQ = 8192
DEPTH = 1
DEC_BATCH = 32
DEC_SEQ = 32
PAST_LEN = 4096

CHUNK = 64
MIX_W = D_MODEL
RET_HEADS = 4
RET_DK = 256
RET_DV = 256
RET_W = RET_HEADS * RET_DV
MLA_HEADS = 8
Q_LORA = 512
KV_LORA = 512
NOPE_DIM = 128
ROPE_DIM = 64
V_HEAD = 128
QK_HEAD = NOPE_DIM + ROPE_DIM
MLA_W = MLA_HEADS * V_HEAD
ROPE_BASE = 10000.0
Q_BLOCK = 128
NEG_INF = -1e30
N_KEYS = 128
N_EXPERTS = N_KEYS * N_KEYS
PEER_HEADS = 8
PEER_QDIM = 256
PEER_HALF = PEER_QDIM // 2
PEER_TOPK = 16
PEER_BLOCK = 128
PLE_DIM = 256
EPS = 1e-6
OFF_RQ = 0
OFF_RK = OFF_RQ + RET_HEADS * RET_DK
OFF_RV = OFF_RK + RET_HEADS * RET_DK
OFF_RG = OFF_RV + RET_W
OFF_CQ = OFF_RG + RET_W
OFF_CKV = OFF_CQ + Q_LORA
OFF_KR = OFF_CKV + KV_LORA
IN_W = OFF_KR + ROPE_DIM

kernel_name = 'hymba_retnet_mla_peer_stream_step'


def rmsnorm(x, g):
    xf = x.astype(jnp.float32)
    y = xf * lax.rsqrt(jnp.mean(xf * xf, axis=-1, keepdims=True) + EPS)
    return (y * g.astype(jnp.float32)).astype(x.dtype)


def rope(x, pos):
    d = x.shape[-1]
    inv = ROPE_BASE ** (-jnp.arange(0, d, 2, dtype=jnp.float32) / d)
    ang = pos.astype(jnp.float32)[:, None] * inv[None, :]
    cos = jnp.cos(ang)[:, None, :]
    sin = jnp.sin(ang)[:, None, :]
    xf = x.astype(jnp.float32)
    x1, x2 = xf[..., : d // 2], xf[..., d // 2:]
    return jnp.concatenate([x1 * cos - x2 * sin, x2 * cos + x1 * sin], axis=-1).astype(x.dtype)


def retention_log_decay():
    return jnp.log1p(-jnp.exp(jnp.linspace(math.log(1.0 / 32), math.log(1.0 / 512), RET_HEADS, dtype=jnp.float32)))


def retention_chunk(q, k, v, s_prev, log_g):
    L = q.shape[1]
    idx = jnp.arange(L)
    diff = idx[:, None] - idx[None, :]
    dmask = jnp.where(diff[None] >= 0,
                      jnp.exp(jnp.maximum(diff, 0).astype(jnp.float32)[None] * log_g[:, None, None]), 0.0)
    scores = jnp.einsum('blhd,bmhd->bhlm', q, k) * dmask[None]
    intra = jnp.einsum('bhlm,bmhe->blhe', scores, v)
    inner = jnp.exp((idx + 1).astype(jnp.float32)[:, None] * log_g[None, :])
    cross = jnp.einsum('blhd,bhde->blhe', q, s_prev) * inner[None, :, :, None]
    kdec = jnp.exp((L - 1 - idx).astype(jnp.float32)[:, None] * log_g[None, :])
    s_new = jnp.exp(L * log_g)[None, :, None, None] * s_prev + \
        jnp.einsum('blhd,blhe->bhde', k * kdec[None, :, :, None], v)
    return intra + cross, s_new


def retention(q, k, v, s0):
    B, T, H, _ = q.shape
    L = min(T, CHUNK)
    nc = T // L
    log_g = retention_log_decay()

    def to_chunks(t):
        return t.reshape(B, nc, L, H, t.shape[-1]).transpose(1, 0, 2, 3, 4)

    def step(s, qkv):
        qc, kc, vc = qkv
        o, s = retention_chunk(qc, kc, vc, s, log_g)
        return s, o

    s_fin, o = lax.scan(step, s0, (to_chunks(q), to_chunks(k), to_chunks(v)))
    o = o.transpose(1, 0, 2, 3, 4).reshape(B, T, H, v.shape[-1])
    return o, s_fin


def head_groupnorm(o, g):
    mu = jnp.mean(o, axis=-1, keepdims=True)
    var = jnp.mean(jnp.square(o - mu), axis=-1, keepdims=True)
    y = (o - mu) * lax.rsqrt(var + EPS)
    B, T = o.shape[:2]
    return y.reshape(B, T, -1) * g.astype(jnp.float32)


def mla_prompt_attention(q_nope, q_rope, c_kv, k_rope, w_uk, w_uv):
    B, S, H, _ = q_nope.shape
    k_nope = jnp.einsum('bsc,chd->bshd', c_kv, w_uk)
    v = jnp.einsum('bsc,chd->bshd', c_kv, w_uv)
    nb = S // Q_BLOCK
    k_chunk = jnp.arange(S) // CHUNK
    scale = QK_HEAD ** -0.5

    def blk(args):
        qn, qr, start = args
        s = (jnp.einsum('bqhd,bkhd->bhqk', qn, k_nope) +
             jnp.einsum('bqhr,bkr->bhqk', qr, k_rope)).astype(jnp.float32) * scale
        q_chunk = (start + jnp.arange(Q_BLOCK)) // CHUNK
        mask = k_chunk[None, :] <= q_chunk[:, None]
        p = jax.nn.softmax(jnp.where(mask[None, None], s, NEG_INF), axis=-1)
        return jnp.einsum('bhqk,bkhd->bqhd', p.astype(v.dtype), v)

    qn_b = q_nope.reshape(B, nb, Q_BLOCK, H, NOPE_DIM).transpose(1, 0, 2, 3, 4)
    qr_b = q_rope.reshape(B, nb, Q_BLOCK, H, ROPE_DIM).transpose(1, 0, 2, 3, 4)
    starts = jnp.arange(nb, dtype=jnp.int32) * Q_BLOCK
    o = lax.map(blk, (qn_b, qr_b, starts))
    return o.transpose(1, 0, 2, 3, 4).reshape(B, S, H, V_HEAD)


def mla_sample_attention(q_nope, q_rope, c_new, kr_new, c_cache, kr_cache, w_uk, w_uv):
    scale = QK_HEAD ** -0.5
    P = c_cache.shape[1]
    q_lat = jnp.einsum('bthd,chd->bthc', q_nope, w_uk)
    s_past = jnp.einsum('bthc,bpc->bhtp', q_lat, c_cache) + jnp.einsum('bthr,bpr->bhtp', q_rope, kr_cache)
    s_new = jnp.einsum('bthc,bnc->bhtn', q_lat, c_new) + jnp.einsum('bthr,bnr->bhtn', q_rope, kr_new)
    s = jnp.concatenate([s_past, s_new], axis=-1).astype(jnp.float32) * scale
    p = jax.nn.softmax(s, axis=-1)
    ctx = jnp.einsum('bhtp,bpc->bthc', p[..., :P], c_cache.astype(jnp.float32)) + \
        jnp.einsum('bhtn,bnc->bthc', p[..., P:], c_new.astype(jnp.float32))
    return jnp.einsum('bthc,chd->bthd', ctx.astype(q_nope.dtype), w_uv)


def peer(m, w_pq, sub_keys, u_tab, v_tab):
    N, D = m.shape
    q = (m @ w_pq).reshape(N, PEER_HEADS, 2, PEER_HALF).astype(jnp.float32)
    sk = sub_keys.astype(jnp.float32)
    s1 = jnp.einsum('nhd,hkd->nhk', q[:, :, 0], sk[:, 0])
    s2 = jnp.einsum('nhd,hkd->nhk', q[:, :, 1], sk[:, 1])
    v1, i1 = lax.top_k(s1, PEER_TOPK)
    v2, i2 = lax.top_k(s2, PEER_TOPK)
    cand = (v1[..., :, None] + v2[..., None, :]).reshape(N, PEER_HEADS, PEER_TOPK * PEER_TOPK)
    cand_idx = (i1[..., :, None] * N_KEYS + i2[..., None, :]).reshape(N, PEER_HEADS, PEER_TOPK * PEER_TOPK)
    top_s, sel = lax.top_k(cand, PEER_TOPK)
    idx = jnp.take_along_axis(cand_idx, sel, axis=-1)
    g = jax.nn.softmax(top_s, axis=-1)
    pad = (-N) % PEER_BLOCK
    nb = (N + pad) // PEER_BLOCK
    m_p = jnp.pad(m, ((0, pad), (0, 0))).reshape(nb, PEER_BLOCK, D)
    idx_p = jnp.pad(idx, ((0, pad), (0, 0), (0, 0))).reshape(nb, PEER_BLOCK, PEER_HEADS, PEER_TOPK)
    g_p = jnp.pad(g, ((0, pad), (0, 0), (0, 0))).reshape(nb, PEER_BLOCK, PEER_HEADS, PEER_TOPK)

    def blk(args):
        mb, ib, gb = args
        u = u_tab[ib]
        h = jnp.einsum('nd,nhkd->nhk', mb, u).astype(jnp.float32)
        a = (gb * jax.nn.gelu(h, approximate=False)).astype(mb.dtype)
        return jnp.einsum('nhk,nhkd->nd', a, v_tab[ib])

    out = lax.map(blk, (m_p, idx_p, g_p))
    return out.reshape(nb * PEER_BLOCK, D)[:N]


def layer(h, p_l, pos, s0, mla_cache, w):
    (g_mix, w_in, g_q, w_uq, g_kv, w_uk, w_uv, g_ret_out, g_mla_out, w_out,
     g_ffn, w_pq, sub_keys, u_tab, v_tab, g_ple, w_ple_gate, w_ple_proj) = w
    B, T, D = h.shape
    a = rmsnorm(h, g_mix)
    z = a @ w_in
    rq = rope(z[..., OFF_RQ:OFF_RK].reshape(B, T, RET_HEADS, RET_DK), pos).astype(jnp.float32)
    rk = rope(z[..., OFF_RK:OFF_RV].reshape(B, T, RET_HEADS, RET_DK), pos).astype(jnp.float32) * (RET_DK ** -0.5)
    rv = z[..., OFF_RV:OFF_RG].reshape(B, T, RET_HEADS, RET_DV).astype(jnp.float32)
    rg = z[..., OFF_RG:OFF_CQ].astype(jnp.float32)
    o_ret, s_new = retention(rq, rk, rv, s0.astype(jnp.float32))
    o_ret = (jax.nn.silu(rg) * head_groupnorm(o_ret, g_ret_out)).astype(h.dtype)
    c_q = rmsnorm(z[..., OFF_CQ:OFF_CKV], g_q)
    qh = jnp.einsum('btc,chd->bthd', c_q, w_uq)
    q_nope = qh[..., :NOPE_DIM]
    q_rope = rope(qh[..., NOPE_DIM:], pos)
    c_kv = rmsnorm(z[..., OFF_CKV:OFF_KR], g_kv)
    k_rope = rope(z[..., OFF_KR:IN_W][:, :, None, :], pos)[:, :, 0, :]
    if mla_cache is None:
        o_mla = mla_prompt_attention(q_nope, q_rope, c_kv, k_rope, w_uk, w_uv)
    else:
        o_mla = mla_sample_attention(q_nope, q_rope, c_kv, k_rope, mla_cache[0], mla_cache[1], w_uk, w_uv)
    o_mla = rmsnorm(o_mla.reshape(B, T, MLA_W).astype(h.dtype), g_mla_out)
    h = h + jnp.concatenate([o_ret, o_mla], axis=-1) @ w_out
    h = h + peer(rmsnorm(h, g_ffn).reshape(B * T, D), w_pq, sub_keys, u_tab, v_tab).reshape(B, T, D)
    gate = jax.nn.sigmoid((rmsnorm(h, g_ple) @ w_ple_gate).astype(jnp.float32))
    h = h + (gate * (p_l @ w_ple_proj).astype(jnp.float32)).astype(h.dtype)
    return h, c_kv, k_rope, s_new.astype(h.dtype)


def setup_inputs(seed: int = 0) -> dict:
    key = jax.random.key(seed)
    ks = jax.random.split(key, 32)
    f32 = jnp.float32

    def nrm(k, shape, scale):
        return jax.random.normal(k, shape, f32) * scale

    def gain(k, shape):
        return 1.0 + 0.01 * jax.random.normal(k, shape, f32)

    return {
        'x_prompt': nrm(ks[0], (BATCH, SEQ, D_MODEL), 1.0),
        'x_sample': nrm(ks[1], (DEC_BATCH, DEC_SEQ, D_MODEL), 1.0),
        'cache_ckv': nrm(ks[2], (DEPTH, DEC_BATCH, PAST_LEN, KV_LORA), 1.0),
        'cache_krope': nrm(ks[3], (DEPTH, DEC_BATCH, PAST_LEN, ROPE_DIM), 1.0),
        'state_ret': nrm(ks[4], (DEPTH, DEC_BATCH, RET_HEADS, RET_DK, RET_DV), 0.1),
        'p_prompt': nrm(ks[5], (DEPTH, BATCH, SEQ, PLE_DIM), 1.0),
        'p_sample': nrm(ks[6], (DEPTH, DEC_BATCH, DEC_SEQ, PLE_DIM), 1.0),
        'g_mix': gain(ks[7], (DEPTH, D_MODEL)),
        'w_in': nrm(ks[8], (DEPTH, D_MODEL, IN_W), D_MODEL ** -0.5),
        'g_q': gain(ks[9], (DEPTH, Q_LORA)),
        'w_uq': nrm(ks[10], (DEPTH, Q_LORA, MLA_HEADS, QK_HEAD), Q_LORA ** -0.5),
        'g_kv': gain(ks[11], (DEPTH, KV_LORA)),
        'w_uk': nrm(ks[12], (DEPTH, KV_LORA, MLA_HEADS, NOPE_DIM), KV_LORA ** -0.5),
        'w_uv': nrm(ks[13], (DEPTH, KV_LORA, MLA_HEADS, V_HEAD), KV_LORA ** -0.5),
        'g_ret_out': gain(ks[14], (DEPTH, RET_W)),
        'g_mla_out': gain(ks[15], (DEPTH, MLA_W)),
        'w_out': nrm(ks[16], (DEPTH, MIX_W, D_MODEL), MIX_W ** -0.5),
        'g_ffn': gain(ks[17], (DEPTH, D_MODEL)),
        'w_pq': nrm(ks[18], (DEPTH, D_MODEL, PEER_HEADS * PEER_QDIM), D_MODEL ** -0.5),
        'sub_keys': nrm(ks[19], (DEPTH, PEER_HEADS, 2, N_KEYS, PEER_HALF), PEER_HALF ** -0.5),
        'u_tab': nrm(ks[20], (DEPTH, N_EXPERTS, D_MODEL), D_MODEL ** -0.5),
        'v_tab': nrm(ks[21], (DEPTH, N_EXPERTS, D_MODEL), 0.05),
        'g_ple': gain(ks[22], (DEPTH, D_MODEL)),
        'w_ple_gate': nrm(ks[23], (DEPTH, D_MODEL, D_MODEL), D_MODEL ** -0.5),
        'w_ple_proj': nrm(ks[24], (DEPTH, PLE_DIM, D_MODEL), PLE_DIM ** -0.5),
        'g_final': gain(ks[25], (D_MODEL,)),
    }


def reference(x_prompt, x_sample, cache_ckv, cache_krope, state_ret, p_prompt, p_sample,
              g_mix, w_in, g_q, w_uq, g_kv, w_uk, w_uv, g_ret_out, g_mla_out, w_out,
              g_ffn, w_pq, sub_keys, u_tab, v_tab, g_ple, w_ple_gate, w_ple_proj, g_final):
    Bp, Sp, _ = x_prompt.shape
    Bs, Ts, _ = x_sample.shape
    P = cache_ckv.shape[2]
    pos_p = jnp.arange(Sp, dtype=jnp.int32)
    pos_s = P + jnp.arange(Ts, dtype=jnp.int32)
    hp, hs = x_prompt, x_sample
    ckv_p, kr_p, ret_p, ckv_s, kr_s, ret_s = [], [], [], [], [], []
    for i in range(DEPTH):
        w = (g_mix[i], w_in[i], g_q[i], w_uq[i], g_kv[i], w_uk[i], w_uv[i], g_ret_out[i],
             g_mla_out[i], w_out[i], g_ffn[i], w_pq[i], sub_keys[i], u_tab[i], v_tab[i],
             g_ple[i], w_ple_gate[i], w_ple_proj[i])
        s0 = jnp.zeros((Bp, RET_HEADS, RET_DK, RET_DV), jnp.float32)
        hp, c1, k1, r1 = layer(hp, p_prompt[i], pos_p, s0, None, w)
        hs, c2, k2, r2 = layer(hs, p_sample[i], pos_s, state_ret[i], (cache_ckv[i], cache_krope[i]), w)
        ckv_p.append(c1); kr_p.append(k1); ret_p.append(r1)
        ckv_s.append(c2); kr_s.append(k2); ret_s.append(r2)
    y_prompt = rmsnorm(hp, g_final)
    y_sample = rmsnorm(hs, g_final)
    return (y_prompt, y_sample, jnp.stack(ckv_p), jnp.stack(kr_p), jnp.stack(ret_p),
            jnp.stack(ckv_s), jnp.stack(kr_s), jnp.stack(ret_s))
```

```python
import functools
import math

import jax
import jax.numpy as jnp
from jax import lax
from jax.experimental import pallas as pl
from jax.experimental.pallas import tpu as pltpu

F32 = jnp.float32
BF16 = jnp.bfloat16

CHUNK = 64
RET_HEADS = 4
RET_DK = 256
RET_DV = 256
MLA_HEADS = 8
NOPE_DIM = 128
ROPE_DIM = 64
V_HEAD = 128
QK_HEAD = NOPE_DIM + ROPE_DIM
ROPE_BASE = 10000.0
PEER_HEADS = 8
PEER_TOPK = 16
N_KEYS = 128
EPS = 1e-6
NEG_INF = -1e30

VMEM_LIMIT_BYTES = 56 * 1024 * 1024
LANES = 128

NT_DIMS = (((1,), (1,)), ((), ()))
TN_DIMS = (((0,), (0,)), ((), ()))


def _cparams(semantics):
    return pltpu.CompilerParams(dimension_semantics=semantics, vmem_limit_bytes=VMEM_LIMIT_BYTES)


def _resident(shape):
    nd = len(shape)
    return pl.BlockSpec(shape, lambda *_: (0,) * nd, pipeline_mode=pl.Buffered(1))


def _rms(x, g):
    return x * lax.rsqrt(jnp.mean(x * x, axis=-1, keepdims=True) + EPS) * g


def _inproj_kernel(x_ref, gmix_ref, win_ref, cr_ref, sr_ref, qmul_ref, kmul_ref, gq_ref, wuq_ref,
                   gkv_ref, rq_ref, rk_ref, rv_ref, rg_ref, q_ref, ckv_ref, kr_ref):
    a = _rms(x_ref[...], gmix_ref[...]).astype(BF16)
    z = jnp.dot(a, win_ref[...], preferred_element_type=F32)
    cr = cr_ref[...]
    sr = sr_ref[...]
    ret_w = RET_HEADS * RET_DK
    half = RET_DK // 2
    for off, ref, scale in ((0, rq_ref, None), (ret_w, rk_ref, RET_DK ** -0.5)):
        for h in range(RET_HEADS):
            lo = off + h * RET_DK
            x1 = z[:, lo:lo + half]
            x2 = z[:, lo + half:lo + RET_DK]
            o1 = x1 * cr - x2 * sr
            o2 = x2 * cr + x1 * sr
            if scale is not None:
                o1 = o1 * scale
                o2 = o2 * scale
            ref[:, h * RET_DK:h * RET_DK + half] = o1.astype(BF16)
            ref[:, h * RET_DK + half:(h + 1) * RET_DK] = o2.astype(BF16)
    rv_ref[...] = z[:, 2 * ret_w:3 * ret_w].astype(BF16)
    rg_ref[...] = z[:, 3 * ret_w:4 * ret_w]
    off_cq = 4 * ret_w
    q_lora = gq_ref.shape[-1]
    kv_lora = gkv_ref.shape[-1]
    cq = _rms(z[:, off_cq:off_cq + q_lora], gq_ref[...]).astype(BF16)
    qh = jnp.dot(cq, wuq_ref[...], preferred_element_type=F32)
    qmul = qmul_ref[...]
    hw = 2 * LANES
    for h in range(MLA_HEADS):
        q_ref[:, h * hw:(h + 1) * hw] = (qh[:, h * hw:(h + 1) * hw] * qmul).astype(BF16)
    off_ckv = off_cq + q_lora
    ckv_ref[...] = _rms(z[:, off_ckv:off_ckv + kv_lora], gkv_ref[...])
    off_kr = off_ckv + kv_lora
    t = z[:, off_kr:off_kr + 2 * ROPE_DIM] * kmul_ref[...]
    kr_ref[...] = t[:, :ROPE_DIM] + t[:, ROPE_DIM:]


def _inproj(x, g_mix, w_in_ext, cos_r, sin_r, qmul, kmul, g_q, w_uq_ext, g_kv, tm):
    n, d = x.shape
    ret_w = RET_HEADS * RET_DK
    q_lora = g_q.shape[-1]
    kv_lora = g_kv.shape[-1]
    qw = MLA_HEADS * 2 * LANES
    row = lambda w: pl.BlockSpec((tm, w), lambda i: (i, 0))
    out_shape = (
        jax.ShapeDtypeStruct((n, ret_w), BF16), jax.ShapeDtypeStruct((n, ret_w), BF16),
        jax.ShapeDtypeStruct((n, ret_w), BF16), jax.ShapeDtypeStruct((n, ret_w), F32),
        jax.ShapeDtypeStruct((n, qw), BF16), jax.ShapeDtypeStruct((n, kv_lora), F32),
        jax.ShapeDtypeStruct((n, ROPE_DIM), F32),
    )
    return pl.pallas_call(
        _inproj_kernel,
        grid=(n // tm,),
        in_specs=[row(d), _resident((1, d)), _resident(w_in_ext.shape), row(RET_DK // 2), row(RET_DK // 2),
                  row(2 * LANES), row(2 * ROPE_DIM), _resident((1, q_lora)), _resident(w_uq_ext.shape),
                  _resident((1, kv_lora))],
        out_specs=(row(ret_w), row(ret_w), row(ret_w), row(ret_w), row(qw), row(kv_lora), row(ROPE_DIM)),
        out_shape=out_shape,
        compiler_params=_cparams(("parallel",)),
        name="inproj",
    )(x, g_mix, w_in_ext, cos_r, sin_r, qmul, kmul, g_q, w_uq_ext, g_kv)


def _kvup_kernel(ckv_ref, kr_ref, wuk_ref, wuv_ref, k_ref, v_ref):
    c = ckv_ref[...].astype(BF16)
    kn = jnp.dot(c, wuk_ref[...], preferred_element_type=F32)
    v_ref[...] = jnp.dot(c, wuv_ref[...], preferred_element_type=F32).astype(BF16)
    kr = kr_ref[...]
    krkr = jnp.concatenate([kr, kr], axis=-1).astype(BF16)
    hw = 2 * LANES
    for h in range(MLA_HEADS):
        k_ref[:, h * hw:h * hw + NOPE_DIM] = kn[:, h * NOPE_DIM:(h + 1) * NOPE_DIM].astype(BF16)
        k_ref[:, h * hw + NOPE_DIM:(h + 1) * hw] = krkr


def _kvup(c_kv, k_rope, w_uk2, w_uv2, n_rows, tm):
    kv_lora = c_kv.shape[-1]
    kw = MLA_HEADS * 2 * LANES
    vw = MLA_HEADS * V_HEAD
    row = lambda w: pl.BlockSpec((tm, w), lambda i: (i, 0))
    return pl.pallas_call(
        _kvup_kernel,
        grid=(n_rows // tm,),
        in_specs=[row(kv_lora), row(ROPE_DIM), _resident(w_uk2.shape), _resident(w_uv2.shape)],
        out_specs=(row(kw), row(vw)),
        out_shape=(jax.ShapeDtypeStruct((n_rows, kw), BF16), jax.ShapeDtypeStruct((n_rows, vw), BF16)),
        compiler_params=_cparams(("parallel",)),
        name="kvup",
    )(c_kv, k_rope, w_uk2, w_uv2)


def _ret_kernel(q_ref, k_ref, v_ref, rg_ref, g_ref, s0_ref, dm_ref, inn_ref, kd_ref, gl_ref,
                o_ref, sn_ref, s_scr, *, nc):
    c = pl.program_id(2)

    @pl.when(c == 0)
    def _():
        s_scr[...] = s0_ref[...]

    q = q_ref[...]
    k = k_ref[...]
    v = v_ref[...]
    s = s_scr[...]
    scores = lax.dot_general(q, k, NT_DIMS, preferred_element_type=F32) * dm_ref[...]
    intra = jnp.dot(scores.astype(BF16), v, preferred_element_type=F32)
    cross = jnp.dot(q, s.astype(BF16), preferred_element_type=F32) * inn_ref[...]
    o = intra + cross
    kd = (k.astype(F32) * kd_ref[...]).astype(BF16)
    s_new = gl_ref[...] * s + lax.dot_general(kd, v, TN_DIMS, preferred_element_type=F32)
    s_scr[...] = s_new

    @pl.when(c == nc - 1)
    def _():
        sn_ref[...] = s_new

    mu = jnp.mean(o, axis=-1, keepdims=True)
    d = o - mu
    var = jnp.mean(d * d, axis=-1, keepdims=True)
    y = d * lax.rsqrt(var + EPS) * g_ref[...]
    rg = rg_ref[...]
    o_ref[...] = (rg * jax.nn.sigmoid(rg) * y).astype(BF16)


def _ret_tables(length):
    log_g = jnp.log1p(-jnp.exp(jnp.linspace(math.log(1.0 / 32), math.log(1.0 / 512), RET_HEADS, dtype=F32)))
    idx = jnp.arange(length)
    diff = idx[:, None] - idx[None, :]
    dmask = jnp.where(diff[None] >= 0,
                      jnp.exp(jnp.maximum(diff, 0).astype(F32)[None] * log_g[:, None, None]), 0.0)
    inner = jnp.exp((idx + 1).astype(F32)[None, :] * log_g[:, None])
    kdec = jnp.exp((length - 1 - idx).astype(F32)[None, :] * log_g[:, None])
    gl = jnp.exp(length * log_g)
    bc = lambda t, w: jnp.broadcast_to(t[..., None], t.shape + (w,))
    return dmask, bc(inner, RET_DV), bc(kdec, RET_DK), bc(gl[:, None], RET_DV)


def _retention(rq, rk, rv, rg, g_ret, s0, row0, n_batch, seq, length):
    nc = seq // length
    blk0 = row0 // length
    dmask, inner, kdec, gl = _ret_tables(length)
    tok = lambda w: pl.BlockSpec((length, w), lambda b, h, c: (blk0 + b * nc + c, h))
    head3 = lambda a: pl.BlockSpec((None,) + a.shape[1:], lambda b, h, c: (h, 0, 0))
    state = pl.BlockSpec((None, None, RET_DK, RET_DV), lambda b, h, c: (b, h, 0, 0))
    return pl.pallas_call(
        functools.partial(_ret_kernel, nc=nc),
        grid=(n_batch, RET_HEADS, nc),
        in_specs=[tok(RET_DK), tok(RET_DK), tok(RET_DV), tok(RET_DV),
                  pl.BlockSpec((1, RET_DV), lambda b, h, c: (0, h)), state,
                  head3(dmask), head3(inner), head3(kdec), head3(gl)],
        out_specs=(pl.BlockSpec((length, RET_DV), lambda b, h, c: (b * nc + c, h)), state),
        out_shape=(jax.ShapeDtypeStruct((n_batch * seq, RET_HEADS * RET_DV), BF16),
                   jax.ShapeDtypeStruct((n_batch, RET_HEADS, RET_DK, RET_DV), F32)),
        scratch_shapes=[pltpu.VMEM((RET_DK, RET_DV), F32)],
        compiler_params=_cparams(("parallel", "parallel", "arbitrary")),
        name="retention",
    )(rq, rk, rv, rg, g_ret, s0, dmask, inner, kdec, gl)


def _flash_kernel(q_ref, k_ref, v_ref, o_ref, m_scr, l_scr, acc_scr, *, tq, scale):
    i = pl.program_id(2)
    q = q_ref[...]
    m_scr[...] = jnp.full(m_scr.shape, NEG_INF, F32)
    l_scr[...] = jnp.zeros(l_scr.shape, F32)
    acc_scr[...] = jnp.zeros(acc_scr.shape, F32)

    def step(kb, vb, mask):
        s = lax.dot_general(q, kb, NT_DIMS, preferred_element_type=F32) * scale
        if mask is not None:
            s = jnp.where(mask, s, NEG_INF)
        m_old = m_scr[...]
        m_new = jnp.maximum(m_old, jnp.max(s, axis=-1, keepdims=True))
        alpha = jnp.exp(m_old - m_new)
        p = jnp.exp(s - m_new)
        l_scr[...] = alpha * l_scr[...] + jnp.sum(p, axis=-1, keepdims=True)
        acc_scr[...] = alpha * acc_scr[...] + jnp.dot(p.astype(BF16), vb, preferred_element_type=F32)
        m_scr[...] = m_new

    def body(j, carry):
        off = pl.multiple_of(j * tq, tq)
        step(k_ref[pl.ds(off, tq), :], v_ref[pl.ds(off, tq), :], None)
        return carry

    lax.fori_loop(0, i, body, 0)
    off = pl.multiple_of(i * tq, tq)
    shift = CHUNK.bit_length() - 1
    rc = lax.shift_right_logical(lax.broadcasted_iota(jnp.int32, (tq, tq), 0), shift)
    kc = lax.shift_right_logical(lax.broadcasted_iota(jnp.int32, (tq, tq), 1), shift)
    step(k_ref[pl.ds(off, tq), :], v_ref[pl.ds(off, tq), :], kc <= rc)
    o_ref[...] = acc_scr[...] / l_scr[...]


def _flash(q_ext, k_ext, v, n_batch, seq, tq):
    hw = 2 * LANES
    nq = seq // tq
    return pl.pallas_call(
        functools.partial(_flash_kernel, tq=tq, scale=QK_HEAD ** -0.5),
        grid=(n_batch, MLA_HEADS, nq),
        in_specs=[pl.BlockSpec((tq, hw), lambda b, h, i: (b * nq + i, h)),
                  pl.BlockSpec((seq, hw), lambda b, h, i: (b, h)),
                  pl.BlockSpec((seq, V_HEAD), lambda b, h, i: (b, h))],
        out_specs=pl.BlockSpec((tq, V_HEAD), lambda b, h, i: (b * nq + i, h)),
        out_shape=jax.ShapeDtypeStruct((n_batch * seq, MLA_HEADS * V_HEAD), F32),
        scratch_shapes=[pltpu.VMEM((tq, 1), F32), pltpu.VMEM((tq, 1), F32), pltpu.VMEM((tq, V_HEAD), F32)],
        compiler_params=_cparams(("parallel", "parallel", "arbitrary")),
        name="flash",
    )(q_ext, k_ext, v)


def _sample_attn_kernel(q_ref, cn_ref, krn_ref, cc_ref, krc_ref, wukt_ref, wuv_ref, o_ref, *, scale):
    ts = q_ref.shape[0]
    hw = 2 * LANES
    qx = q_ref[...]
    q_lat, q_rp = [], []
    for h in range(MLA_HEADS):
        nope = qx[:, h * hw:h * hw + NOPE_DIM]
        q_lat.append(jnp.dot(nope, wukt_ref[h], preferred_element_type=F32).astype(BF16))
        q_rp.append(qx[:, h * hw + NOPE_DIM:(h + 1) * hw])
    q_lat = jnp.concatenate(q_lat, axis=0)
    q_rp = jnp.concatenate(q_rp, axis=0)

    def keys(c_ref, kr_ref):
        kr = kr_ref[...]
        return c_ref[...].astype(BF16), jnp.concatenate([kr, kr], axis=-1).astype(BF16)

    c_past, kk_past = keys(cc_ref, krc_ref)
    c_new, kk_new = keys(cn_ref, krn_ref)

    def scores(c, kk):
        return (lax.dot_general(q_lat, c, NT_DIMS, preferred_element_type=F32)
                + lax.dot_general(q_rp, kk, NT_DIMS, preferred_element_type=F32)) * scale

    s_past = scores(c_past, kk_past)
    s_new = scores(c_new, kk_new)
    m = jnp.maximum(jnp.max(s_past, axis=-1, keepdims=True), jnp.max(s_new, axis=-1, keepdims=True))
    p_past = jnp.exp(s_past - m)
    p_new = jnp.exp(s_new - m)
    denom = jnp.sum(p_past, axis=-1, keepdims=True) + jnp.sum(p_new, axis=-1, keepdims=True)
    ctx = (jnp.dot(p_past.astype(BF16), c_past, preferred_element_type=F32)
           + jnp.dot(p_new.astype(BF16), c_new, preferred_element_type=F32)) / denom
    ctx = ctx.astype(BF16)
    for h in range(MLA_HEADS):
        o_ref[:, h * V_HEAD:(h + 1) * V_HEAD] = jnp.dot(ctx[h * ts:(h + 1) * ts], wuv_ref[h],
                                                        preferred_element_type=F32)


def _sample_attn(q_ext, c_kv, k_rope, cache_ckv, cache_krope, w_ukt, w_uvh, row0, n_batch, ts):
    past = cache_ckv.shape[1]
    kv_lora = cache_ckv.shape[2]
    blk0 = row0 // ts
    new = lambda w: pl.BlockSpec((ts, w), lambda b: (blk0 + b, 0))
    return pl.pallas_call(
        functools.partial(_sample_attn_kernel, scale=QK_HEAD ** -0.5),
        grid=(n_batch,),
        in_specs=[new(MLA_HEADS * 2 * LANES), new(kv_lora), new(ROPE_DIM),
                  pl.BlockSpec((None, past, kv_lora), lambda b: (b, 0, 0)),
                  pl.BlockSpec((None, past, ROPE_DIM), lambda b: (b, 0, 0)),
                  _resident(w_ukt.shape), _resident(w_uvh.shape)],
        out_specs=pl.BlockSpec((ts, MLA_HEADS * V_HEAD), lambda b: (b, 0)),
        out_shape=jax.ShapeDtypeStruct((n_batch * ts, MLA_HEADS * V_HEAD), F32),
        compiler_params=_cparams(("parallel",)),
        name="sample_attn",
    )(q_ext, c_kv, k_rope, cache_ckv, cache_krope, w_ukt, w_uvh)


def _outproj_kernel(x_ref, oret_ref, omla_ref, gmla_ref, wout_ref, gffn_ref, wpq_ref, sk_ref,
                    h1_ref, mt_ref, st_ref):
    ret_w = oret_ref.shape[-1]
    om = _rms(omla_ref[...], gmla_ref[...]).astype(BF16)
    y = (jnp.dot(oret_ref[...], wout_ref[:ret_w, :], preferred_element_type=F32)
         + jnp.dot(om, wout_ref[ret_w:, :], preferred_element_type=F32))
    h1 = x_ref[...] + y
    h1_ref[...] = h1
    m = _rms(h1, gffn_ref[...])
    mt_ref[...] = m.T.astype(BF16)
    qq = jnp.dot(m.astype(BF16), wpq_ref[...], preferred_element_type=F32)
    kd = sk_ref.shape[-1]
    for hh in range(sk_ref.shape[0]):
        qh = qq[:, hh * kd:(hh + 1) * kd].astype(BF16)
        st_ref[hh] = lax.dot_general(sk_ref[hh], qh, NT_DIMS, preferred_element_type=F32)


def _outproj(x, o_ret, o_mla, g_mla, w_out, g_ffn, w_pq, sk, tm):
    n, d = x.shape
    nh, nkeys, _ = sk.shape
    row = lambda w: pl.BlockSpec((tm, w), lambda i: (i, 0))
    return pl.pallas_call(
        _outproj_kernel,
        grid=(n // tm,),
        in_specs=[row(d), row(o_ret.shape[-1]), row(o_mla.shape[-1]), _resident(g_mla.shape),
                  _resident(w_out.shape), _resident(g_ffn.shape), _resident(w_pq.shape), _resident(sk.shape)],
        out_specs=(row(d), pl.BlockSpec((d, tm), lambda i: (0, i)),
                   pl.BlockSpec((nh, nkeys, tm), lambda i: (0, 0, i))),
        out_shape=(jax.ShapeDtypeStruct((n, d), F32), jax.ShapeDtypeStruct((d, n), BF16),
                   jax.ShapeDtypeStruct((nh, nkeys, n), F32)),
        compiler_params=_cparams(("parallel",)),
        name="outproj",
    )(x, o_ret, o_mla, g_mla, w_out, g_ffn, w_pq, sk)


def _extract_topk(s, iota_f):
    rank = jnp.full(s.shape, float(PEER_TOPK), F32)
    rows = []
    for a in range(PEER_TOPK):
        mx = jnp.max(s, axis=0, keepdims=True)
        idx = jnp.min(jnp.where(s == mx, iota_f, float(N_KEYS)), axis=0, keepdims=True)
        hit = iota_f == idx
        rank = jnp.where(hit, float(a), rank)
        s = jnp.where(hit, -jnp.inf, s)
        rows.append(mx)
    return rows, rank


def _stack_rows(rows, t):
    io = lax.broadcasted_iota(jnp.int32, (len(rows), t), 0)
    out = jnp.zeros((len(rows), t), F32)
    for a, r in enumerate(rows):
        out = jnp.where(io == a, r, out)
    return out


def _topk_kernel(st_ref, r2_ref, nb_ref, e1_ref, e2_ref):
    t = st_ref.shape[-1]
    k = PEER_TOPK
    s1 = st_ref[0]
    s2 = st_ref[1]
    iota_f = lax.broadcasted_iota(jnp.int32, s1.shape, 0).astype(F32)
    v1_rows, rank1 = _extract_topk(s1, iota_f)
    v2_rows, rank2 = _extract_topk(s2, iota_f)
    v1 = _stack_rows(v1_rows, t)
    v2 = _stack_rows(v2_rows, t)
    e1v = jnp.exp(v1 - v1_rows[0])
    e2v = jnp.exp(v2 - v2_rows[0])
    e2_rows = [jnp.exp(r - v2_rows[0]) for r in v2_rows]

    io8 = lax.broadcasted_iota(jnp.int32, (8, t), 0)
    io16 = lax.broadcasted_iota(jnp.int32, (k, t), 0)
    cand = [v1 + v2_rows[0]]
    prod = [e1v]
    flat = [(io16 * k).astype(F32)]
    for b in range(1, 8):
        ok = io8 < (k // (b + 1))
        cand.append(jnp.where(ok, v1[:8] + v2_rows[b], -jnp.inf))
        prod.append(e1v[:8] * e2_rows[b])
        flat.append((io8 * k + b).astype(F32))
    cand.append(v1_rows[0] + v2[8:])
    prod.append(e2v[8:])
    flat.append((io8 + 8).astype(F32))
    cand = jnp.concatenate(cand, axis=0)
    prod = jnp.concatenate(prod, axis=0)
    flat = jnp.concatenate(flat, axis=0)

    sel = jnp.zeros(cand.shape, F32)
    for _ in range(k):
        mx = jnp.max(cand, axis=0, keepdims=True)
        idx = jnp.min(jnp.where(cand == mx, flat, 1e9), axis=0, keepdims=True)
        hit = flat == idx
        sel = jnp.where(hit, 1.0, sel)
        cand = jnp.where(hit, -jnp.inf, cand)
    z = jnp.sum(sel * prod, axis=0, keepdims=True)

    low = sel[0:8]
    for b in range(1, 8):
        low = low + sel[8 + 8 * b:16 + 8 * b]
    low = low + jnp.where(io8 == 0, jnp.sum(sel[72:80], axis=0, keepdims=True), 0.0)
    nb = jnp.concatenate([low, sel[8:16]], axis=0)
    nbd = jnp.zeros(s1.shape, F32)
    for a in range(k):
        nbd = jnp.where(rank1 == float(a), nb[a:a + 1, :], nbd)

    r2_ref[...] = rank2
    nb_ref[...] = nbd
    e1_ref[...] = jnp.exp(s1 - v1_rows[0]) / z
    e2_ref[...] = jnp.exp(s2 - v2_rows[0])


def _topk(st, tt):
    nh2, nkeys, n = st.shape
    nh = nh2 // 2
    blk = pl.BlockSpec((None, nkeys, tt), lambda h, t: (h, 0, t))
    shp = jax.ShapeDtypeStruct((nh, nkeys, n), F32)
    return pl.pallas_call(
        _topk_kernel,
        grid=(nh, n // tt),
        in_specs=[pl.BlockSpec((2, nkeys, tt), lambda h, t: (h, 0, t))],
        out_specs=(blk, blk, blk, blk),
        out_shape=(shp, shp, shp, shp),
        compiler_params=_cparams(("parallel", "parallel")),
        name="topk",
    )(st)


def _gelu_exact(x):
    return 0.5 * x * (1.0 + lax.erf(x * (0.5 ** 0.5)))


def _peer_kernel(mt_ref, u_ref, vt_ref, r2_ref, e2_ref, nb_ref, e1_ref, out_ref, acc_ref, h_scr, g_scr,
                 *, n_steps):
    c = pl.program_id(1)
    tn = mt_ref.shape[-1]
    nj = nb_ref.shape[1]

    @pl.when(c == 0)
    def _():
        acc_ref[...] = jnp.zeros(acc_ref.shape, F32)

    h_scr[...] = jnp.dot(u_ref[...], mt_ref[...], preferred_element_type=F32)

    def lane_chunk(lc, carry):
        off = pl.multiple_of(lc * LANES, LANES)
        cols = pl.ds(off, LANES)
        for j in range(nj):
            a = jnp.zeros((N_KEYS, LANES), F32)
            for h in range(PEER_HEADS):
                nb_row = nb_ref[h, j:j + 1, cols]
                e1_row = e1_ref[h, j:j + 1, cols]
                a = a + jnp.where(r2_ref[h, :, cols] < nb_row, e1_row * e2_ref[h, :, cols], 0.0)
            hj = h_scr[j * N_KEYS:(j + 1) * N_KEYS, cols]
            g_scr[j * N_KEYS:(j + 1) * N_KEYS, cols] = (a * _gelu_exact(hj)).astype(BF16)
        return carry

    lax.fori_loop(0, tn // LANES, lane_chunk, 0)
    acc_ref[...] += jnp.dot(vt_ref[...], g_scr[...], preferred_element_type=F32)

    @pl.when(c == n_steps - 1)
    def _():
        out_ref[...] = acc_ref[...].T


def _peer(m_t, u_bf, vt_bf, r2, e2, nbd, e1, tn, key_group):
    d, n = m_t.shape
    nh, nkeys, _ = r2.shape
    ec = key_group * nkeys
    n_steps = nkeys // key_group
    nbd = nbd.reshape(nh, n_steps, key_group, n)
    e1 = e1.reshape(nh, n_steps, key_group, n)
    key2 = pl.BlockSpec((nh, nkeys, tn), lambda t, c: (0, 0, t))
    key1 = pl.BlockSpec((nh, None, key_group, tn), lambda t, c: (0, c, 0, t))
    return pl.pallas_call(
        functools.partial(_peer_kernel, n_steps=n_steps),
        grid=(n // tn, n_steps),
        in_specs=[pl.BlockSpec((d, tn), lambda t, c: (0, t)),
                  pl.BlockSpec((ec, d), lambda t, c: (c, 0)),
                  pl.BlockSpec((d, ec), lambda t, c: (0, c)),
                  key2, key2, key1, key1],
        out_specs=pl.BlockSpec((tn, d), lambda t, c: (t, 0)),
        out_shape=jax.ShapeDtypeStruct((n, d), F32),
        scratch_shapes=[pltpu.VMEM((d, tn), F32), pltpu.VMEM((ec, tn), F32), pltpu.VMEM((ec, tn), BF16)],
        compiler_params=_cparams(("parallel", "arbitrary")),
        name="peer",
    )(m_t, u_bf, vt_bf, r2, e2, nbd, e1)


def _ple_kernel(h1_ref, po_ref, p_ref, gple_ref, wg_ref, wp_ref, h_ref):
    h2 = h1_ref[...] + po_ref[...]
    gate = jax.nn.sigmoid(jnp.dot(_rms(h2, gple_ref[...]).astype(BF16), wg_ref[...],
                                  preferred_element_type=F32))
    pp = jnp.dot(p_ref[...].astype(BF16), wp_ref[...], preferred_element_type=F32)
    h_ref[...] = h2 + gate * pp


def _ple(h1, peer_out, p, g_ple, w_gate, w_proj, tm):
    n, d = h1.shape
    row = lambda w: pl.BlockSpec((tm, w), lambda i: (i, 0))
    return pl.pallas_call(
        _ple_kernel,
        grid=(n // tm,),
        in_specs=[row(d), row(d), row(p.shape[-1]), _resident(g_ple.shape), _resident(w_gate.shape),
                  _resident(w_proj.shape)],
        out_specs=row(d),
        out_shape=jax.ShapeDtypeStruct((n, d), F32),
        compiler_params=_cparams(("parallel",)),
        name="ple",
    )(h1, peer_out, p, g_ple, w_gate, w_proj)


def _final_kernel(h_ref, g_ref, y_ref):
    y_ref[...] = _rms(h_ref[...], g_ref[...])


def _final_norm(h, g, tm):
    n, d = h.shape
    row = pl.BlockSpec((tm, d), lambda i: (i, 0))
    return pl.pallas_call(
        _final_kernel,
        grid=(n // tm,),
        in_specs=[row, _resident(g.shape)],
        out_specs=row,
        out_shape=jax.ShapeDtypeStruct((n, d), F32),
        compiler_params=_cparams(("parallel",)),
        name="final_norm",
    )(h, g)


def _rope_tables(pos, dim):
    inv = ROPE_BASE ** (-jnp.arange(0, dim, 2, dtype=F32) / dim)
    ang = pos.astype(F32)[:, None] * inv[None, :]
    return jnp.cos(ang), jnp.sin(ang)


def _rot_cols(w):
    half = w.shape[-1] // 2
    return jnp.concatenate([-w[..., half:], w[..., :half]], axis=-1)


def _tile(n, pref):
    t = pref
    while n % t:
        t //= 2
    return t


def kernel(x_prompt, x_sample, cache_ckv, cache_krope, state_ret, p_prompt, p_sample, g_mix, w_in, g_q, w_uq,
           g_kv, w_uk, w_uv, g_ret_out, g_mla_out, w_out, g_ffn, w_pq, sub_keys, u_tab, v_tab, g_ple,
           w_ple_gate, w_ple_proj, g_final):
    bp, sp, d = x_prompt.shape
    bs, ts, _ = x_sample.shape
    depth, _, past, kv_lora = cache_ckv.shape
    q_lora = g_q.shape[-1]
    n_p, n_s = bp * sp, bs * ts
    n = n_p + n_s
    ret_w = RET_HEADS * RET_DK
    off_kr = 4 * ret_w + q_lora + kv_lora

    tm = _tile(math.gcd(n_p, n_s), 256)
    tq = _tile(sp, 512)
    ret_len = _tile(sp, 256)
    tn = _tile(n, 512)
    tt = _tile(n, 256)
    key_group = 8

    pos = jnp.concatenate([jnp.tile(jnp.arange(sp, dtype=jnp.int32), bp),
                           jnp.tile(past + jnp.arange(ts, dtype=jnp.int32), bs)])
    cos_r, sin_r = _rope_tables(pos, RET_DK)
    cos_m, sin_m = _rope_tables(pos, ROPE_DIM)
    cos_m2 = jnp.concatenate([cos_m, cos_m], axis=-1)
    sin_m2 = jnp.concatenate([sin_m, sin_m], axis=-1)
    kmul = jnp.concatenate([cos_m2, sin_m2], axis=-1)
    qmul = jnp.concatenate([jnp.ones((n, NOPE_DIM), F32), kmul], axis=-1)

    h = jnp.concatenate([x_prompt.reshape(n_p, d), x_sample.reshape(n_s, d)], axis=0)
    outs = [[] for _ in range(6)]
    for i in range(depth):
        w_in_i = w_in[i]
        w_in_ext = jnp.concatenate([w_in_i, _rot_cols(w_in_i[:, off_kr:])], axis=-1).astype(BF16)
        wq = w_uq[i]
        w_uq_ext = jnp.concatenate([wq, _rot_cols(wq[..., NOPE_DIM:])], axis=-1)
        w_uq_ext = w_uq_ext.reshape(q_lora, MLA_HEADS * 2 * LANES).astype(BF16)
        w_uk2 = w_uk[i].reshape(kv_lora, MLA_HEADS * NOPE_DIM).astype(BF16)
        w_uv2 = w_uv[i].reshape(kv_lora, MLA_HEADS * V_HEAD).astype(BF16)
        w_ukt = jnp.transpose(w_uk[i], (1, 2, 0)).astype(BF16)
        w_uvh = jnp.transpose(w_uv[i], (1, 0, 2)).astype(BF16)
        sk = sub_keys[i].reshape(2 * PEER_HEADS, N_KEYS, -1).astype(BF16)
        row2 = lambda g: g[i].reshape(1, -1)

        rq, rk, rv, rg, q_ext, c_kv, k_rope = _inproj(
            h, row2(g_mix), w_in_ext, cos_r, sin_r, qmul, kmul, row2(g_q), w_uq_ext, row2(g_kv), tm)

        g_ret = row2(g_ret_out)
        s0_p = jnp.zeros((bp, RET_HEADS, RET_DK, RET_DV), F32)
        oret_p, sfin_p = _retention(rq, rk, rv, rg, g_ret, s0_p, 0, bp, sp, ret_len)
        oret_s, sfin_s = _retention(rq, rk, rv, rg, g_ret, state_ret[i], n_p, bs, ts, min(ts, CHUNK))

        k_ext, v = _kvup(c_kv, k_rope, w_uk2, w_uv2, n_p, tm)
        omla_p = _flash(q_ext, k_ext, v, bp, sp, tq)
        omla_s = _sample_attn(q_ext, c_kv, k_rope, cache_ckv[i], cache_krope[i], w_ukt, w_uvh, n_p, bs, ts)

        o_ret = jnp.concatenate([oret_p, oret_s], axis=0)
        o_mla = jnp.concatenate([omla_p, omla_s], axis=0)
        h1, m_t, st = _outproj(h, o_ret, o_mla, row2(g_mla_out), w_out[i].astype(BF16), row2(g_ffn),
                               w_pq[i].astype(BF16), sk, tm)
        r2, nbd, e1, e2 = _topk(st, tt)
        peer_out = _peer(m_t, u_tab[i].astype(BF16), v_tab[i].T.astype(BF16), r2, e2, nbd, e1, tn, key_group)
        p_l = jnp.concatenate([p_prompt[i].reshape(n_p, -1), p_sample[i].reshape(n_s, -1)], axis=0)
        h = _ple(h1, peer_out, p_l, row2(g_ple), w_ple_gate[i].astype(BF16), w_ple_proj[i].astype(BF16), tm)

        outs[0].append(c_kv[:n_p].reshape(bp, sp, kv_lora))
        outs[1].append(k_rope[:n_p].reshape(bp, sp, ROPE_DIM))
        outs[2].append(sfin_p)
        outs[3].append(c_kv[n_p:].reshape(bs, ts, kv_lora))
        outs[4].append(k_rope[n_p:].reshape(bs, ts, ROPE_DIM))
        outs[5].append(sfin_s)

    y = _final_norm(h, g_final.reshape(1, d), tm)
    y_prompt = y[:n_p].reshape(bp, sp, d)
    y_sample = y[n_p:].reshape(bs, ts, d)
    return (y_prompt, y_sample) + tuple(jnp.stack(o) for o in outs)
```

```python
import functools
import math

import jax
import jax.numpy as jnp
from jax import lax
from jax.experimental import pallas as pl
from jax.experimental.pallas import tpu as pltpu

F32 = jnp.float32
BF16 = jnp.bfloat16

CHUNK = 64
RET_HEADS = 4
RET_DK = 256
RET_DV = 256
MLA_HEADS = 8
NOPE_DIM = 128
ROPE_DIM = 64
V_HEAD = 128
QK_HEAD = NOPE_DIM + ROPE_DIM
ROPE_BASE = 10000.0
PEER_HEADS = 8
PEER_TOPK = 16
N_KEYS = 128
EPS = 1e-6
NEG_INF = -1e30

VMEM_LIMIT_BYTES = 56 * 1024 * 1024
LANES = 128

NT_DIMS = (((1,), (1,)), ((), ()))
TN_DIMS = (((0,), (0,)), ((), ()))


def _cparams(semantics):
    return pltpu.CompilerParams(dimension_semantics=semantics, vmem_limit_bytes=VMEM_LIMIT_BYTES)


def _resident(shape):
    nd = len(shape)
    return pl.BlockSpec(shape, lambda *_: (0,) * nd, pipeline_mode=pl.Buffered(1))


def _rms(x, g):
    return x * lax.rsqrt(jnp.mean(x * x, axis=-1, keepdims=True) + EPS) * g


def _inproj_kernel(x_ref, gmix_ref, win_ref, cr_ref, sr_ref, qmul_ref, kmul_ref, gq_ref, wuq_ref,
                   gkv_ref, rq_ref, rk_ref, rv_ref, rg_ref, q_ref, ckv_ref, kr_ref):
    a = _rms(x_ref[...], gmix_ref[...]).astype(BF16)
    z = jnp.dot(a, win_ref[...], preferred_element_type=F32)
    cr = cr_ref[...]
    sr = sr_ref[...]
    ret_w = RET_HEADS * RET_DK
    half = RET_DK // 2
    for off, ref, scale in ((0, rq_ref, None), (ret_w, rk_ref, RET_DK ** -0.5)):
        for h in range(RET_HEADS):
            lo = off + h * RET_DK
            x1 = z[:, lo:lo + half]
            x2 = z[:, lo + half:lo + RET_DK]
            o1 = x1 * cr - x2 * sr
            o2 = x2 * cr + x1 * sr
            if scale is not None:
                o1 = o1 * scale
                o2 = o2 * scale
            ref[:, h * RET_DK:h * RET_DK + half] = o1.astype(BF16)
            ref[:, h * RET_DK + half:(h + 1) * RET_DK] = o2.astype(BF16)
    rv_ref[...] = z[:, 2 * ret_w:3 * ret_w].astype(BF16)
    rg_ref[...] = z[:, 3 * ret_w:4 * ret_w]
    off_cq = 4 * ret_w
    q_lora = gq_ref.shape[-1]
    kv_lora = gkv_ref.shape[-1]
    cq = _rms(z[:, off_cq:off_cq + q_lora], gq_ref[...]).astype(BF16)
    qh = jnp.dot(cq, wuq_ref[...], preferred_element_type=F32)
    qmul = qmul_ref[...]
    hw = 2 * LANES
    for h in range(MLA_HEADS):
        q_ref[:, h * hw:(h + 1) * hw] = (qh[:, h * hw:(h + 1) * hw] * qmul).astype(BF16)
    off_ckv = off_cq + q_lora
    ckv_ref[...] = _rms(z[:, off_ckv:off_ckv + kv_lora], gkv_ref[...])
    off_kr = off_ckv + kv_lora
    t = z[:, off_kr:off_kr + 2 * ROPE_DIM] * kmul_ref[...]
    kr_ref[...] = t[:, :ROPE_DIM] + t[:, ROPE_DIM:]


def _inproj(x, g_mix, w_in_ext, cos_r, sin_r, qmul, kmul, g_q, w_uq_ext, g_kv, tm):
    n, d = x.shape
    ret_w = RET_HEADS * RET_DK
    q_lora = g_q.shape[-1]
    kv_lora = g_kv.shape[-1]
    qw = MLA_HEADS * 2 * LANES
    row = lambda w: pl.BlockSpec((tm, w), lambda i: (i, 0))
    out_shape = (
        jax.ShapeDtypeStruct((n, ret_w), BF16), jax.ShapeDtypeStruct((n, ret_w), BF16),
        jax.ShapeDtypeStruct((n, ret_w), BF16), jax.ShapeDtypeStruct((n, ret_w), F32),
        jax.ShapeDtypeStruct((n, qw), BF16), jax.ShapeDtypeStruct((n, kv_lora), F32),
        jax.ShapeDtypeStruct((n, ROPE_DIM), F32),
    )
    return pl.pallas_call(
        _inproj_kernel,
        grid=(n // tm,),
        in_specs=[row(d), _resident((1, d)), _resident(w_in_ext.shape), row(RET_DK // 2), row(RET_DK // 2),
                  row(2 * LANES), row(2 * ROPE_DIM), _resident((1, q_lora)), _resident(w_uq_ext.shape),
                  _resident((1, kv_lora))],
        out_specs=(row(ret_w), row(ret_w), row(ret_w), row(ret_w), row(qw), row(kv_lora), row(ROPE_DIM)),
        out_shape=out_shape,
        compiler_params=_cparams(("parallel",)),
        name="inproj",
    )(x, g_mix, w_in_ext, cos_r, sin_r, qmul, kmul, g_q, w_uq_ext, g_kv)


def _kvup_kernel(ckv_ref, kr_ref, wuk_ref, wuv_ref, k_ref, vt_ref):
    c = ckv_ref[...].astype(BF16)
    kn = jnp.dot(c, wuk_ref[...], preferred_element_type=F32)
    v = jnp.dot(c, wuv_ref[...], preferred_element_type=F32)
    kr = kr_ref[...]
    krkr = jnp.concatenate([kr, kr], axis=-1).astype(BF16)
    hw = 2 * LANES
    for h in range(MLA_HEADS):
        k_ref[:, h * hw:h * hw + NOPE_DIM] = kn[:, h * NOPE_DIM:(h + 1) * NOPE_DIM].astype(BF16)
        k_ref[:, h * hw + NOPE_DIM:(h + 1) * hw] = krkr
        vt_ref[h] = v[:, h * V_HEAD:(h + 1) * V_HEAD].T.astype(BF16)


def _kvup(c_kv, k_rope, w_uk2, w_uv2, n_rows, tk):
    kv_lora = c_kv.shape[-1]
    kw = MLA_HEADS * 2 * LANES
    row = lambda w: pl.BlockSpec((tk, w), lambda i: (i, 0))
    return pl.pallas_call(
        _kvup_kernel,
        grid=(n_rows // tk,),
        in_specs=[row(kv_lora), row(ROPE_DIM), _resident(w_uk2.shape), _resident(w_uv2.shape)],
        out_specs=(row(kw), pl.BlockSpec((None, MLA_HEADS, V_HEAD, tk), lambda i: (i, 0, 0, 0))),
        out_shape=(jax.ShapeDtypeStruct((n_rows, kw), BF16),
                   jax.ShapeDtypeStruct((n_rows // tk, MLA_HEADS, V_HEAD, tk), BF16)),
        compiler_params=_cparams(("parallel",)),
        name="kvup",
    )(c_kv, k_rope, w_uk2, w_uv2)


def _ret_kernel(q_ref, k_ref, v_ref, rg_ref, g_ref, s0_ref, dm_ref, inn_ref, kd_ref, gl_ref,
                o_ref, sn_ref, s_scr, *, nc):
    c = pl.program_id(2)

    @pl.when(c == 0)
    def _():
        s_scr[...] = s0_ref[...]

    q = q_ref[...]
    k = k_ref[...]
    v = v_ref[...]
    s = s_scr[...]
    scores = lax.dot_general(q, k, NT_DIMS, preferred_element_type=F32) * dm_ref[...]
    intra = jnp.dot(scores.astype(BF16), v, preferred_element_type=F32)
    cross = jnp.dot(q, s.astype(BF16), preferred_element_type=F32) * inn_ref[...]
    o = intra + cross
    kd = (k.astype(F32) * kd_ref[...]).astype(BF16)
    s_new = gl_ref[...] * s + lax.dot_general(kd, v, TN_DIMS, preferred_element_type=F32)
    s_scr[...] = s_new

    @pl.when(c == nc - 1)
    def _():
        sn_ref[...] = s_new

    mu = jnp.mean(o, axis=-1, keepdims=True)
    d = o - mu
    var = jnp.mean(d * d, axis=-1, keepdims=True)
    y = d * lax.rsqrt(var + EPS) * g_ref[...]
    rg = rg_ref[...]
    o_ref[...] = (rg * jax.nn.sigmoid(rg) * y).astype(BF16)


def _ret_tables(length):
    log_g = jnp.log1p(-jnp.exp(jnp.linspace(math.log(1.0 / 32), math.log(1.0 / 512), RET_HEADS, dtype=F32)))
    idx = jnp.arange(length)
    diff = idx[:, None] - idx[None, :]
    dmask = jnp.where(diff[None] >= 0,
                      jnp.exp(jnp.maximum(diff, 0).astype(F32)[None] * log_g[:, None, None]), 0.0)
    inner = jnp.exp((idx + 1).astype(F32)[None, :] * log_g[:, None])
    kdec = jnp.exp((length - 1 - idx).astype(F32)[None, :] * log_g[:, None])
    gl = jnp.exp(length * log_g)
    bc = lambda t, w: jnp.broadcast_to(t[..., None], t.shape + (w,))
    return dmask, bc(inner, RET_DV), bc(kdec, RET_DK), bc(gl[:, None], RET_DV)


def _retention(rq, rk, rv, rg, g_ret, s0, row0, n_batch, seq, length):
    nc = seq // length
    blk0 = row0 // length
    dmask, inner, kdec, gl = _ret_tables(length)
    tok = lambda w: pl.BlockSpec((length, w), lambda b, h, c: (blk0 + b * nc + c, h))
    head3 = lambda a: pl.BlockSpec((None,) + a.shape[1:], lambda b, h, c: (h, 0, 0))
    state = pl.BlockSpec((None, None, RET_DK, RET_DV), lambda b, h, c: (b, h, 0, 0))
    return pl.pallas_call(
        functools.partial(_ret_kernel, nc=nc),
        grid=(n_batch, RET_HEADS, nc),
        in_specs=[tok(RET_DK), tok(RET_DK), tok(RET_DV), tok(RET_DV),
                  pl.BlockSpec((1, RET_DV), lambda b, h, c: (0, h)), state,
                  head3(dmask), head3(inner), head3(kdec), head3(gl)],
        out_specs=(pl.BlockSpec((length, RET_DV), lambda b, h, c: (b * nc + c, h)), state),
        out_shape=(jax.ShapeDtypeStruct((n_batch * seq, RET_HEADS * RET_DV), BF16),
                   jax.ShapeDtypeStruct((n_batch, RET_HEADS, RET_DK, RET_DV), F32)),
        scratch_shapes=[pltpu.VMEM((RET_DK, RET_DV), F32)],
        compiler_params=_cparams(("parallel", "parallel", "arbitrary")),
        name="retention",
    )(rq, rk, rv, rg, g_ret, s0, dmask, inner, kdec, gl)


def _flash_kernel(q_ref, k_ref, vt_ref, o_ref, qt_scr, s_even, s_odd, m_scr, l_scr, acc_scr, *, tq):
    i = pl.program_id(2)
    qt_scr[...] = q_ref[...].astype(F32).T.astype(BF16)
    m_scr[...] = jnp.full(m_scr.shape, NEG_INF, F32)
    l_scr[...] = jnp.zeros(l_scr.shape, F32)
    acc_scr[...] = jnp.zeros(acc_scr.shape, F32)

    def scores(j):
        off = pl.multiple_of(j * tq, tq)
        return jnp.dot(k_ref[pl.ds(off, tq), :], qt_scr[...], preferred_element_type=F32)

    def update(s, j):
        m_old = m_scr[...]
        m_new = jnp.maximum(m_old, jnp.max(s, axis=0, keepdims=True))
        alpha = jnp.exp(m_old - m_new)
        p = jnp.exp(s - m_new)
        l_scr[...] = alpha * l_scr[...] + jnp.sum(p, axis=0, keepdims=True)
        acc_scr[...] = alpha * acc_scr[...] + jnp.dot(vt_ref[j], p.astype(BF16), preferred_element_type=F32)
        m_scr[...] = m_new

    def diagonal(s):
        shift = CHUNK.bit_length() - 1
        kc = lax.shift_right_logical(lax.broadcasted_iota(jnp.int32, (tq, tq), 0), shift)
        qc = lax.shift_right_logical(lax.broadcasted_iota(jnp.int32, (tq, tq), 1), shift)
        return jnp.where(kc <= qc, s, NEG_INF)

    s_even[...] = scores(0)

    def pair(p, carry):
        j = 2 * p
        s_odd[...] = scores(j + 1)
        update(s_even[...], j)
        s_even[...] = scores(j + 2)
        update(s_odd[...], j + 1)
        return carry

    lax.fori_loop(0, i // 2, pair, 0)

    @pl.when(lax.rem(i, 2) == 0)
    def _():
        update(diagonal(s_even[...]), i)

    @pl.when(lax.rem(i, 2) == 1)
    def _():
        s_odd[...] = scores(i)
        update(s_even[...], i - 1)
        update(diagonal(s_odd[...]), i)

    o_ref[...] = (acc_scr[...] / l_scr[...]).T


def _flash(q_ext, k_ext, v_t, n_batch, seq, tq):
    hw = 2 * LANES
    nq = seq // tq
    return pl.pallas_call(
        functools.partial(_flash_kernel, tq=tq),
        grid=(n_batch, MLA_HEADS, nq),
        in_specs=[pl.BlockSpec((tq, hw), lambda b, h, i: (b * nq + i, h)),
                  pl.BlockSpec((seq, hw), lambda b, h, i: (b, h)),
                  pl.BlockSpec((nq, None, V_HEAD, tq), lambda b, h, i: (b, h, 0, 0))],
        out_specs=pl.BlockSpec((tq, V_HEAD), lambda b, h, i: (b * nq + i, h)),
        out_shape=jax.ShapeDtypeStruct((n_batch * seq, MLA_HEADS * V_HEAD), F32),
        scratch_shapes=[pltpu.VMEM((hw, tq), BF16), pltpu.VMEM((tq, tq), F32), pltpu.VMEM((tq, tq), F32),
                        pltpu.VMEM((1, tq), F32), pltpu.VMEM((1, tq), F32), pltpu.VMEM((V_HEAD, tq), F32)],
        compiler_params=_cparams(("parallel", "parallel", "arbitrary")),
        name="flash",
    )(q_ext, k_ext, v_t)


def _sample_attn_kernel(q_ref, cn_ref, krn_ref, cc_ref, krc_ref, wukt_ref, wuv_ref, o_ref):
    ts = q_ref.shape[0]
    hw = 2 * LANES
    qx = q_ref[...]
    q_lat, q_rp = [], []
    for h in range(MLA_HEADS):
        nope = qx[:, h * hw:h * hw + NOPE_DIM]
        q_lat.append(jnp.dot(nope, wukt_ref[h], preferred_element_type=F32).astype(BF16))
        q_rp.append(qx[:, h * hw + NOPE_DIM:(h + 1) * hw])
    q_lat = jnp.concatenate(q_lat, axis=0)
    q_rp = jnp.concatenate(q_rp, axis=0)

    def keys(c_ref, kr_ref):
        kr = kr_ref[...]
        return c_ref[...].astype(BF16), jnp.concatenate([kr, kr], axis=-1).astype(BF16)

    c_past, kk_past = keys(cc_ref, krc_ref)
    c_new, kk_new = keys(cn_ref, krn_ref)

    def scores(c, kk):
        return (lax.dot_general(q_lat, c, NT_DIMS, preferred_element_type=F32)
                + lax.dot_general(q_rp, kk, NT_DIMS, preferred_element_type=F32))

    s_past = scores(c_past, kk_past)
    s_new = scores(c_new, kk_new)
    m = jnp.maximum(jnp.max(s_past, axis=-1, keepdims=True), jnp.max(s_new, axis=-1, keepdims=True))
    p_past = jnp.exp(s_past - m)
    p_new = jnp.exp(s_new - m)
    denom = jnp.sum(p_past, axis=-1, keepdims=True) + jnp.sum(p_new, axis=-1, keepdims=True)
    ctx = (jnp.dot(p_past.astype(BF16), c_past, preferred_element_type=F32)
           + jnp.dot(p_new.astype(BF16), c_new, preferred_element_type=F32)) / denom
    ctx = ctx.astype(BF16)
    for h in range(MLA_HEADS):
        o_ref[:, h * V_HEAD:(h + 1) * V_HEAD] = jnp.dot(ctx[h * ts:(h + 1) * ts], wuv_ref[h],
                                                        preferred_element_type=F32)


def _sample_attn(q_ext, c_kv, k_rope, cache_ckv, cache_krope, w_ukt, w_uvh, row0, n_batch, ts):
    past = cache_ckv.shape[1]
    kv_lora = cache_ckv.shape[2]
    blk0 = row0 // ts
    new = lambda w: pl.BlockSpec((ts, w), lambda b: (blk0 + b, 0))
    return pl.pallas_call(
        _sample_attn_kernel,
        grid=(n_batch,),
        in_specs=[new(MLA_HEADS * 2 * LANES), new(kv_lora), new(ROPE_DIM),
                  pl.BlockSpec((None, past, kv_lora), lambda b: (b, 0, 0)),
                  pl.BlockSpec((None, past, ROPE_DIM), lambda b: (b, 0, 0)),
                  _resident(w_ukt.shape), _resident(w_uvh.shape)],
        out_specs=pl.BlockSpec((ts, MLA_HEADS * V_HEAD), lambda b: (b, 0)),
        out_shape=jax.ShapeDtypeStruct((n_batch * ts, MLA_HEADS * V_HEAD), F32),
        compiler_params=_cparams(("parallel",)),
        name="sample_attn",
    )(q_ext, c_kv, k_rope, cache_ckv, cache_krope, w_ukt, w_uvh)


def _outproj_kernel(x_ref, oret_ref, omla_ref, gmla_ref, wout_ref, gffn_ref, wpq_ref, sk_ref,
                    h1_ref, mt_ref, st_ref):
    ret_w = oret_ref.shape[-1]
    om = _rms(omla_ref[...], gmla_ref[...]).astype(BF16)
    y = (jnp.dot(oret_ref[...], wout_ref[:ret_w, :], preferred_element_type=F32)
         + jnp.dot(om, wout_ref[ret_w:, :], preferred_element_type=F32))
    h1 = x_ref[...] + y
    h1_ref[...] = h1
    m = _rms(h1, gffn_ref[...])
    mt_ref[...] = m.T.astype(BF16)
    qq = jnp.dot(m.astype(BF16), wpq_ref[...], preferred_element_type=F32)
    kd = sk_ref.shape[-1]
    for hh in range(sk_ref.shape[0]):
        qh = qq[:, hh * kd:(hh + 1) * kd].astype(BF16)
        st_ref[hh] = lax.dot_general(sk_ref[hh], qh, NT_DIMS, preferred_element_type=F32)


def _outproj(x, o_ret, o_mla, g_mla, w_out, g_ffn, w_pq, sk, tm):
    n, d = x.shape
    nh, nkeys, _ = sk.shape
    row = lambda w: pl.BlockSpec((tm, w), lambda i: (i, 0))
    return pl.pallas_call(
        _outproj_kernel,
        grid=(n // tm,),
        in_specs=[row(d), row(o_ret.shape[-1]), row(o_mla.shape[-1]), _resident(g_mla.shape),
                  _resident(w_out.shape), _resident(g_ffn.shape), _resident(w_pq.shape), _resident(sk.shape)],
        out_specs=(row(d), pl.BlockSpec((d, tm), lambda i: (0, i)),
                   pl.BlockSpec((nh, nkeys, tm), lambda i: (0, 0, i))),
        out_shape=(jax.ShapeDtypeStruct((n, d), F32), jax.ShapeDtypeStruct((d, n), BF16),
                   jax.ShapeDtypeStruct((nh, nkeys, n), F32)),
        compiler_params=_cparams(("parallel",)),
        name="outproj",
    )(x, o_ret, o_mla, g_mla, w_out, g_ffn, w_pq, sk)


def _extract_topk(s, iota_f):
    rank = jnp.full(s.shape, float(PEER_TOPK), F32)
    rows = []
    for a in range(PEER_TOPK):
        mx = jnp.max(s, axis=0, keepdims=True)
        idx = jnp.min(jnp.where(s == mx, iota_f, float(N_KEYS)), axis=0, keepdims=True)
        hit = iota_f == idx
        rank = jnp.where(hit, float(a), rank)
        s = jnp.where(hit, -jnp.inf, s)
        rows.append(mx)
    return rows, rank


def _stack_rows(rows, t):
    io = lax.broadcasted_iota(jnp.int32, (len(rows), t), 0)
    out = jnp.zeros((len(rows), t), F32)
    for a, r in enumerate(rows):
        out = jnp.where(io == a, r, out)
    return out


def _topk_kernel(st_ref, r2_ref, nb_ref, e1_ref, e2_ref):
    t = st_ref.shape[-1]
    k = PEER_TOPK
    s1 = st_ref[0]
    s2 = st_ref[1]
    iota_f = lax.broadcasted_iota(jnp.int32, s1.shape, 0).astype(F32)
    v1_rows, rank1 = _extract_topk(s1, iota_f)
    v2_rows, rank2 = _extract_topk(s2, iota_f)
    v1 = _stack_rows(v1_rows, t)
    v2 = _stack_rows(v2_rows, t)
    e1v = jnp.exp(v1 - v1_rows[0])
    e2v = jnp.exp(v2 - v2_rows[0])
    e2_rows = [jnp.exp(r - v2_rows[0]) for r in v2_rows]

    io8 = lax.broadcasted_iota(jnp.int32, (8, t), 0)
    io16 = lax.broadcasted_iota(jnp.int32, (k, t), 0)
    cand = [v1 + v2_rows[0]]
    prod = [e1v]
    flat = [(io16 * k).astype(F32)]
    for b in range(1, 8):
        ok = io8 < (k // (b + 1))
        cand.append(jnp.where(ok, v1[:8] + v2_rows[b], -jnp.inf))
        prod.append(e1v[:8] * e2_rows[b])
        flat.append((io8 * k + b).astype(F32))
    cand.append(v1_rows[0] + v2[8:])
    prod.append(e2v[8:])
    flat.append((io8 + 8).astype(F32))
    cand = jnp.concatenate(cand, axis=0)
    prod = jnp.concatenate(prod, axis=0)
    flat = jnp.concatenate(flat, axis=0)

    sel = jnp.zeros(cand.shape, F32)
    for _ in range(k):
        mx = jnp.max(cand, axis=0, keepdims=True)
        idx = jnp.min(jnp.where(cand == mx, flat, 1e9), axis=0, keepdims=True)
        hit = flat == idx
        sel = jnp.where(hit, 1.0, sel)
        cand = jnp.where(hit, -jnp.inf, cand)
    z = jnp.sum(sel * prod, axis=0, keepdims=True)

    low = sel[0:8]
    for b in range(1, 8):
        low = low + sel[8 + 8 * b:16 + 8 * b]
    low = low + jnp.where(io8 == 0, jnp.sum(sel[72:80], axis=0, keepdims=True), 0.0)
    nb = jnp.concatenate([low, sel[8:16]], axis=0)
    nbd = jnp.zeros(s1.shape, F32)
    for a in range(k):
        nbd = jnp.where(rank1 == float(a), nb[a:a + 1, :], nbd)

    e2 = jnp.exp(s2 - v2_rows[0])
    for ck in range(t // LANES):
        r2_ref[ck] = rank2[:, ck * LANES:(ck + 1) * LANES]
        e2_ref[ck] = e2[:, ck * LANES:(ck + 1) * LANES]
    nb_ref[...] = nbd
    e1_ref[...] = jnp.exp(s1 - v1_rows[0]) / z


def _topk(st, tt):
    nh2, nkeys, n = st.shape
    nh = nh2 // 2
    key1 = pl.BlockSpec((None, nkeys, tt), lambda h, t: (h, 0, t))
    key2 = pl.BlockSpec((None, tt // LANES, nkeys, LANES), lambda h, t: (h, t, 0, 0))
    shp1 = jax.ShapeDtypeStruct((nh, nkeys, n), F32)
    shp2 = jax.ShapeDtypeStruct((nh, n // LANES, nkeys, LANES), F32)
    return pl.pallas_call(
        _topk_kernel,
        grid=(nh, n // tt),
        in_specs=[pl.BlockSpec((2, nkeys, tt), lambda h, t: (h, 0, t))],
        out_specs=(key2, key1, key1, key2),
        out_shape=(shp2, shp1, shp1, shp2),
        compiler_params=_cparams(("parallel", "parallel")),
        name="topk",
    )(st)


def _gelu_exact(x):
    return 0.5 * x * (1.0 + lax.erf(x * (0.5 ** 0.5)))


def _peer_kernel(mt_ref, u_ref, vt_ref, r2_ref, e2_ref, nb_ref, e1_ref, out_ref, acc_ref,
                 h_even, h_odd, g_even, g_odd, *, n_steps):
    g = pl.program_id(0)
    tn = mt_ref.shape[-1]
    nj = nb_ref.shape[1]
    parity = lax.rem(g, 2)
    c_mix = lax.rem(jnp.maximum(g - 2, 0), n_steps)

    @pl.when(g == 0)
    def _():
        h_odd[...] = jnp.zeros(h_odd.shape, F32)
        g_even[...] = jnp.zeros(g_even.shape, BF16)

    @pl.when(c_mix == 0)
    def _():
        acc_ref[...] = jnp.zeros(acc_ref.shape, F32)

    def step(h_cur, h_prev, g_prev, g_prev2):
        ec, d = u_ref.shape
        rows_a = 2 * N_KEYS
        rows_c = 2 * N_KEYS

        def stage_a(k):
            rows = slice(k * rows_a, (k + 1) * rows_a)
            h_cur[rows, :] = jnp.dot(u_ref[rows, :], mt_ref[...], preferred_element_type=F32)

        def stage_c(k):
            rows = slice(k * rows_c, (k + 1) * rows_c)
            acc_ref[rows, :] += jnp.dot(vt_ref[rows, :], g_prev2[...], preferred_element_type=F32)

        def stage_b(j, lc):
            cols = slice(lc * LANES, (lc + 1) * LANES)
            a = jnp.zeros((N_KEYS, LANES), F32)
            for h in range(PEER_HEADS):
                nb_row = nb_ref[h, j:j + 1, cols]
                e1_row = e1_ref[h, j:j + 1, cols]
                a = a + jnp.where(r2_ref[h, lc] < nb_row, e1_row * e2_ref[h, lc], 0.0)
            hj = h_prev[j * N_KEYS:(j + 1) * N_KEYS, cols]
            g_prev[j * N_KEYS:(j + 1) * N_KEYS, cols] = (a * _gelu_exact(hj)).astype(BF16)

        mxu_work = []
        n_a, n_c = ec // rows_a, d // rows_c
        for k in range(max(n_a, n_c)):
            if k < n_a:
                mxu_work.append(functools.partial(stage_a, k))
            if k < n_c:
                mxu_work.append(functools.partial(stage_c, k))
        vpu_work = [functools.partial(stage_b, j, lc) for j in range(nj) for lc in range(tn // LANES)]
        done = 0
        for idx, mm in enumerate(mxu_work):
            mm()
            upto = (len(vpu_work) * (idx + 1)) // len(mxu_work)
            for unit in vpu_work[done:upto]:
                unit()
            done = upto

    @pl.when(parity == 0)
    def _():
        step(h_even, h_odd, g_odd, g_even)

    @pl.when(parity == 1)
    def _():
        step(h_odd, h_even, g_even, g_odd)

    @pl.when(jnp.logical_and(g >= 2, c_mix == n_steps - 1))
    def _():
        out_ref[...] = acc_ref[...].T


def _peer(m_t, u_bf, vt_bf, r2, e2, nbd, e1, tn, key_group):
    d, n = m_t.shape
    nh, nkeys, _ = nbd.shape
    ec = key_group * nkeys
    n_steps = nkeys // key_group
    n_items = (n // tn) * n_steps
    nbd = nbd.reshape(nh, n_steps, key_group, n)
    e1 = e1.reshape(nh, n_steps, key_group, n)

    def item(lag):
        def split(g):
            it = jnp.clip(g - lag, 0, n_items - 1)
            return it // n_steps, it % n_steps
        return split

    item_a, item_b, item_c = item(0), item(1), item(2)
    key2 = pl.BlockSpec((nh, tn // LANES, nkeys, LANES), lambda g: (0, item_b(g)[0], 0, 0))
    key1 = pl.BlockSpec((nh, None, key_group, tn), lambda g: (0, item_b(g)[1], 0, item_b(g)[0]))
    return pl.pallas_call(
        functools.partial(_peer_kernel, n_steps=n_steps),
        grid=(n_items + 2,),
        in_specs=[pl.BlockSpec((d, tn), lambda g: (0, item_a(g)[0])),
                  pl.BlockSpec((ec, d), lambda g: (item_a(g)[1], 0)),
                  pl.BlockSpec((d, ec), lambda g: (0, item_c(g)[1])),
                  key2, key2, key1, key1],
        out_specs=pl.BlockSpec((tn, d), lambda g: (item_c(g)[0], 0)),
        out_shape=jax.ShapeDtypeStruct((n, d), F32),
        scratch_shapes=[pltpu.VMEM((d, tn), F32), pltpu.VMEM((ec, tn), F32), pltpu.VMEM((ec, tn), F32),
                        pltpu.VMEM((ec, tn), BF16), pltpu.VMEM((ec, tn), BF16)],
        compiler_params=_cparams(("arbitrary",)),
        name="peer",
    )(m_t, u_bf, vt_bf, r2, e2, nbd, e1)


def _ple_kernel(h1_ref, po_ref, p_ref, gple_ref, wg_ref, wp_ref, h_ref):
    h2 = h1_ref[...] + po_ref[...]
    gate = jax.nn.sigmoid(jnp.dot(_rms(h2, gple_ref[...]).astype(BF16), wg_ref[...],
                                  preferred_element_type=F32))
    pp = jnp.dot(p_ref[...].astype(BF16), wp_ref[...], preferred_element_type=F32)
    h_ref[...] = h2 + gate * pp


def _ple(h1, peer_out, p, g_ple, w_gate, w_proj, tm):
    n, d = h1.shape
    row = lambda w: pl.BlockSpec((tm, w), lambda i: (i, 0))
    return pl.pallas_call(
        _ple_kernel,
        grid=(n // tm,),
        in_specs=[row(d), row(d), row(p.shape[-1]), _resident(g_ple.shape), _resident(w_gate.shape),
                  _resident(w_proj.shape)],
        out_specs=row(d),
        out_shape=jax.ShapeDtypeStruct((n, d), F32),
        compiler_params=_cparams(("parallel",)),
        name="ple",
    )(h1, peer_out, p, g_ple, w_gate, w_proj)


def _final_kernel(h_ref, g_ref, y_ref):
    y_ref[...] = _rms(h_ref[...], g_ref[...])


def _final_norm(h, g, tm):
    n, d = h.shape
    row = pl.BlockSpec((tm, d), lambda i: (i, 0))
    return pl.pallas_call(
        _final_kernel,
        grid=(n // tm,),
        in_specs=[row, _resident(g.shape)],
        out_specs=row,
        out_shape=jax.ShapeDtypeStruct((n, d), F32),
        compiler_params=_cparams(("parallel",)),
        name="final_norm",
    )(h, g)


def _rope_tables(pos, dim):
    inv = ROPE_BASE ** (-jnp.arange(0, dim, 2, dtype=F32) / dim)
    ang = pos.astype(F32)[:, None] * inv[None, :]
    return jnp.cos(ang), jnp.sin(ang)


def _rot_cols(w):
    half = w.shape[-1] // 2
    return jnp.concatenate([-w[..., half:], w[..., :half]], axis=-1)


def _tile(n, pref):
    t = pref
    while n % t:
        t //= 2
    return t


def kernel(x_prompt, x_sample, cache_ckv, cache_krope, state_ret, p_prompt, p_sample, g_mix, w_in, g_q, w_uq,
           g_kv, w_uk, w_uv, g_ret_out, g_mla_out, w_out, g_ffn, w_pq, sub_keys, u_tab, v_tab, g_ple,
           w_ple_gate, w_ple_proj, g_final):
    bp, sp, d = x_prompt.shape
    bs, ts, _ = x_sample.shape
    depth, _, past, kv_lora = cache_ckv.shape
    q_lora = g_q.shape[-1]
    n_p, n_s = bp * sp, bs * ts
    n = n_p + n_s
    ret_w = RET_HEADS * RET_DK
    off_kr = 4 * ret_w + q_lora + kv_lora

    tm = _tile(math.gcd(n_p, n_s), 256)
    tq = _tile(sp, 512)
    ret_len = _tile(sp, 256)
    tn = _tile(n, 512)
    tt = _tile(n, 256)
    key_group = 8

    pos = jnp.concatenate([jnp.tile(jnp.arange(sp, dtype=jnp.int32), bp),
                           jnp.tile(past + jnp.arange(ts, dtype=jnp.int32), bs)])
    cos_r, sin_r = _rope_tables(pos, RET_DK)
    cos_m, sin_m = _rope_tables(pos, ROPE_DIM)
    cos_m2 = jnp.concatenate([cos_m, cos_m], axis=-1)
    sin_m2 = jnp.concatenate([sin_m, sin_m], axis=-1)
    kmul = jnp.concatenate([cos_m2, sin_m2], axis=-1)
    qmul = jnp.concatenate([jnp.ones((n, NOPE_DIM), F32), kmul], axis=-1) * (QK_HEAD ** -0.5)

    h = jnp.concatenate([x_prompt.reshape(n_p, d), x_sample.reshape(n_s, d)], axis=0)
    outs = [[] for _ in range(6)]
    for i in range(depth):
        w_in_i = w_in[i]
        w_in_ext = jnp.concatenate([w_in_i, _rot_cols(w_in_i[:, off_kr:])], axis=-1).astype(BF16)
        wq = w_uq[i]
        w_uq_ext = jnp.concatenate([wq, _rot_cols(wq[..., NOPE_DIM:])], axis=-1)
        w_uq_ext = w_uq_ext.reshape(q_lora, MLA_HEADS * 2 * LANES).astype(BF16)
        w_uk2 = w_uk[i].reshape(kv_lora, MLA_HEADS * NOPE_DIM).astype(BF16)
        w_uv2 = w_uv[i].reshape(kv_lora, MLA_HEADS * V_HEAD).astype(BF16)
        w_ukt = jnp.transpose(w_uk[i], (1, 2, 0)).astype(BF16)
        w_uvh = jnp.transpose(w_uv[i], (1, 0, 2)).astype(BF16)
        sk = sub_keys[i].reshape(2 * PEER_HEADS, N_KEYS, -1).astype(BF16)
        row2 = lambda g: g[i].reshape(1, -1)

        rq, rk, rv, rg, q_ext, c_kv, k_rope = _inproj(
            h, row2(g_mix), w_in_ext, cos_r, sin_r, qmul, kmul, row2(g_q), w_uq_ext, row2(g_kv), tm)

        g_ret = row2(g_ret_out)
        s0_p = jnp.zeros((bp, RET_HEADS, RET_DK, RET_DV), F32)
        oret_p, sfin_p = _retention(rq, rk, rv, rg, g_ret, s0_p, 0, bp, sp, ret_len)
        oret_s, sfin_s = _retention(rq, rk, rv, rg, g_ret, state_ret[i], n_p, bs, ts, min(ts, CHUNK))

        k_ext, v_t = _kvup(c_kv, k_rope, w_uk2, w_uv2, n_p, tq)
        omla_p = _flash(q_ext, k_ext, v_t, bp, sp, tq)
        omla_s = _sample_attn(q_ext, c_kv, k_rope, cache_ckv[i], cache_krope[i], w_ukt, w_uvh, n_p, bs, ts)

        o_ret = jnp.concatenate([oret_p, oret_s], axis=0)
        o_mla = jnp.concatenate([omla_p, omla_s], axis=0)
        h1, m_t, st = _outproj(h, o_ret, o_mla, row2(g_mla_out), w_out[i].astype(BF16), row2(g_ffn),
                               w_pq[i].astype(BF16), sk, tm)
        r2, nbd, e1, e2 = _topk(st, tt)
        peer_out = _peer(m_t, u_tab[i].astype(BF16), v_tab[i].T.astype(BF16), r2, e2, nbd, e1, tn, key_group)
        p_l = jnp.concatenate([p_prompt[i].reshape(n_p, -1), p_sample[i].reshape(n_s, -1)], axis=0)
        h = _ple(h1, peer_out, p_l, row2(g_ple), w_ple_gate[i].astype(BF16), w_ple_proj[i].astype(BF16), tm)

        outs[0].append(c_kv[:n_p].reshape(bp, sp, kv_lora))
        outs[1].append(k_rope[:n_p].reshape(bp, sp, ROPE_DIM))
        outs[2].append(sfin_p)
        outs[3].append(c_kv[n_p:].reshape(bs, ts, kv_lora))
        outs[4].append(k_rope[n_p:].reshape(bs, ts, ROPE_DIM))
        outs[5].append(sfin_s)

    y = _final_norm(h, g_final.reshape(1, d), tm)
    y_prompt = y[:n_p].reshape(bp, sp, d)
    y_sample = y[n_p:].reshape(bs, ts, d)
    return (y_prompt, y_sample) + tuple(jnp.stack(o) for o in outs)
```
